```python
import math
import jax, jax.numpy as jnp
from jax import lax
import numpy as np

D_MODEL = 1024
BATCH = 8
SEQ = 2048
DEPTH = 2
DEC_BATCH = 128
DEC_SEQ = 8
PAST_LEN = 8192
PAGE_SIZE = 128

MLA_HEADS = 4
MLA_NOPE = 128
MLA_ROPE = 64
MLA_V = 128
MLA_Q_LORA = 384
MLA_KV_LORA = 256
MLA_WIDTH = MLA_HEADS * MLA_V
DIFF_HEADS = 4
DIFF_HALF = 32
DIFF_HEAD_DIM = 2 * DIFF_HALF
DIFF_WIDTH = DIFF_HEADS * DIFF_HEAD_DIM
MEM_TOKENS = 256
MEM_HEADS = 4
MEM_HEAD_DIM = 64
MEM_WIDTH = MEM_HEADS * MEM_HEAD_DIM
D_MIX = MLA_WIDTH + DIFF_WIDTH + MEM_WIDTH
IN_COLS = MLA_Q_LORA + MLA_KV_LORA + MLA_ROPE + 3 * DIFF_WIDTH + MEM_WIDTH + D_MIX
REL_BUCKETS = 32
REL_MAX_DIST = 128
ROPE_BASE = 10000.0
Q_BLOCK = 128
MLA_SCALE = (MLA_NOPE + MLA_ROPE) ** -0.5
DIFF_SCALE = DIFF_HALF ** -0.5
MEM_SCALE = MEM_HEAD_DIM ** -0.5
RMS_EPS = 1e-6
LN_EPS = 1e-5
DEEPNORM_ALPHA = (2 * DEPTH) ** 0.25
DEEPNORM_BETA = (8 * DEPTH) ** -0.25

kernel_name = 'hymba_mla_diffattn_memory_decoder_step'


def rms_norm(x, g):
    xf = x.astype(jnp.float32)
    y = xf * lax.rsqrt(jnp.mean(xf * xf, axis=-1, keepdims=True) + RMS_EPS)
    return (y * g.astype(jnp.float32)).astype(x.dtype)


def layer_norm(x, g, b):
    xf = x.astype(jnp.float32)
    mu = jnp.mean(xf, axis=-1, keepdims=True)
    xc = xf - mu
    var = jnp.mean(xc * xc, axis=-1, keepdims=True)
    return (xc * lax.rsqrt(var + LN_EPS) * g.astype(jnp.float32) + b.astype(jnp.float32)).astype(x.dtype)


def rope(x, pos):
    half = MLA_ROPE // 2
    inv = ROPE_BASE ** (-jnp.arange(half, dtype=jnp.float32) * (2.0 / MLA_ROPE))
    ang = pos.astype(jnp.float32)[:, None] * inv[None, :]
    ang = ang.reshape(ang.shape[:1] + (1,) * (x.ndim - 3) + ang.shape[1:])
    cos, sin = jnp.cos(ang), jnp.sin(ang)
    xf = x.astype(jnp.float32)
    x1, x2 = xf[..., :half], xf[..., half:]
    return jnp.concatenate([x1 * cos - x2 * sin, x1 * sin + x2 * cos], axis=-1).astype(x.dtype)


def rel_bucket(q_pos, k_pos):
    n = jnp.maximum(q_pos[:, None] - k_pos[None, :], 0)
    max_exact = REL_BUCKETS // 2
    nf = jnp.maximum(n, 1).astype(jnp.float32)
    large = max_exact + (jnp.log(nf / max_exact) / math.log(REL_MAX_DIST / max_exact)
                         * (REL_BUCKETS - max_exact)).astype(jnp.int32)
    large = jnp.minimum(large, REL_BUCKETS - 1)
    return jnp.where(n < max_exact, n, large)


def masked_softmax(s, q_pos, k_pos):
    mask = k_pos[None, :] <= q_pos[:, None]
    return jax.nn.softmax(jnp.where(mask, s, -jnp.inf), axis=-1)


def mla_core(q_lat, q_rope, ckv, krope, q_pos, k_pos):
    s = (jnp.einsum('bthr,bkr->bhtk', q_lat, ckv).astype(jnp.float32)
         + jnp.einsum('bthp,bkp->bhtk', q_rope, krope).astype(jnp.float32)) * MLA_SCALE
    p = masked_softmax(s, q_pos, k_pos).astype(ckv.dtype)
    return jnp.einsum('bhtk,bkr->bthr', p, ckv)


def diff_core(q, k, v, lam, rel_table, q_pos, k_pos):
    bias = jnp.transpose(rel_table[rel_bucket(q_pos, k_pos)].astype(jnp.float32), (2, 0, 1))

    def probs(qh, kh):
        s = jnp.einsum('bthd,bkhd->bhtk', qh, kh).astype(jnp.float32) * DIFF_SCALE + bias
        return masked_softmax(s, q_pos, k_pos)

    p = (probs(q[..., :DIFF_HALF], k[..., :DIFF_HALF])
         - lam * probs(q[..., DIFF_HALF:], k[..., DIFF_HALF:]))
    return jnp.einsum('bhtk,bkhd->bthd', p.astype(v.dtype), v)


def mem_core(q, k, v):
    s = jnp.einsum('bthd,bmhd->bhtm', q, k).astype(jnp.float32) * MEM_SCALE
    p = jax.nn.softmax(s, axis=-1).astype(v.dtype)
    return jnp.einsum('bhtm,bmhd->bthd', p, v)


def _blocks(a):
    b, s = a.shape[:2]
    return jnp.moveaxis(a.reshape((b, s // Q_BLOCK, Q_BLOCK) + a.shape[2:]), 1, 0)


def _unblocks(a):
    a = jnp.moveaxis(a, 0, 1)
    return a.reshape((a.shape[0], a.shape[1] * a.shape[2]) + a.shape[3:])


def sweep_query_blocks(fn, qs, q_pos):
    qb = tuple(_blocks(a) for a in qs)
    out = lax.map(lambda a: fn(*a[0], a[1]), (qb, q_pos.reshape(-1, Q_BLOCK)))
    return _unblocks(out)


def all_queries(fn, qs, q_pos):
    return fn(*qs, q_pos)


def gather_pages(cache, page_table, l):
    g = cache[page_table, l]
    return g.reshape((g.shape[0], g.shape[1] * g.shape[2]) + g.shape[3:])


def decoder_layer(h, q_pos, k_pos, past, mem_kv, lam, lam_init, attend, params, rel_table):
    w_in, q_norm, w_uq, kv_norm, w_uk, w_uv, diff_norm, w_out, ln_g, ln_b = params
    b, t, _ = h.shape
    sizes = (MLA_Q_LORA, MLA_KV_LORA, MLA_ROPE, DIFF_WIDTH, DIFF_WIDTH, DIFF_WIDTH, MEM_WIDTH, D_MIX)
    cuts = [int(c) for c in np.cumsum(sizes)[:-1]]
    c_q, c_kv, k_r, dq, dk, dv, mq, gate = jnp.split(h @ w_in, cuts, axis=-1)

    q = (rms_norm(c_q, q_norm) @ w_uq).reshape(b, t, MLA_HEADS, MLA_NOPE + MLA_ROPE)
    q_rope = rope(q[..., MLA_NOPE:], q_pos)
    q_lat = jnp.einsum('bthd,rhd->bthr', q[..., :MLA_NOPE], w_uk)
    ckv_new = rms_norm(c_kv, kv_norm)
    kr_new = rope(k_r, q_pos)
    dq = dq.reshape(b, t, DIFF_HEADS, DIFF_HEAD_DIM)
    dk_new = dk.reshape(b, t, DIFF_HEADS, DIFF_HEAD_DIM)
    dv_new = dv.reshape(b, t, DIFF_HEADS, DIFF_HEAD_DIM)
    mq = mq.reshape(b, t, MEM_HEADS, MEM_HEAD_DIM)

    if past is None:
        ckv_all, kr_all, dk_all, dv_all = ckv_new, kr_new, dk_new, dv_new
    else:
        p_ckv, p_kr, p_dk, p_dv = past
        ckv_all = jnp.concatenate([p_ckv, ckv_new], axis=1)
        kr_all = jnp.concatenate([p_kr, kr_new], axis=1)
        dk_all = jnp.concatenate([p_dk, dk_new], axis=1)
        dv_all = jnp.concatenate([p_dv, dv_new], axis=1)

    o_lat = attend(lambda ql, qr, qp: mla_core(ql, qr, ckv_all, kr_all, qp, k_pos), (q_lat, q_rope), q_pos)
    o_mla = jnp.einsum('bthr,rhv->bthv', o_lat, w_uv).reshape(b, t, MLA_WIDTH)

    o_diff = attend(lambda qd, qp: diff_core(qd, dk_all, dv_all, lam, rel_table, qp, k_pos), (dq,), q_pos)
    o_diff = (rms_norm(o_diff, diff_norm) * (1.0 - lam_init)).reshape(b, t, DIFF_WIDTH)

    o_mem = mem_core(mq, mem_kv[0], mem_kv[1]).reshape(b, t, MEM_WIDTH)

    branch = jnp.concatenate([o_mla, o_diff, o_mem], axis=-1) * jax.nn.silu(gate)
    y = branch @ w_out
    return layer_norm(DEEPNORM_ALPHA * h + y, ln_g, ln_b), (ckv_new, kr_new, dk_new, dv_new)


def setup_inputs(seed: int = 0) -> dict:
    key = jax.random.key(seed)
    ks = iter(jax.random.split(key, 40))
    f32 = jnp.float32
    n_pages = PAST_LEN // PAGE_SIZE
    n_phys = (DEC_BATCH * n_pages * 5) // 4

    def nrm(shape, scale=1.0):
        return scale * jax.random.normal(next(ks), shape, f32)

    x_prompt = nrm((BATCH, SEQ, D_MODEL))
    x_sample = nrm((DEC_BATCH, DEC_SEQ, D_MODEL))
    mem_prompt = nrm((BATCH, MEM_TOKENS, D_MODEL))
    cache_mla_ckv = nrm((n_phys, DEPTH, PAGE_SIZE, MLA_KV_LORA))
    cache_mla_krope = nrm((n_phys, DEPTH, PAGE_SIZE, MLA_ROPE))
    cache_diff_k = nrm((n_phys, DEPTH, PAGE_SIZE, DIFF_HEADS, DIFF_HEAD_DIM))
    cache_diff_v = nrm((n_phys, DEPTH, PAGE_SIZE, DIFF_HEADS, DIFF_HEAD_DIM), DEEPNORM_BETA)
    cache_mem_k = nrm((DEC_BATCH, DEPTH, MEM_TOKENS, MEM_HEADS, MEM_HEAD_DIM))
    cache_mem_v = nrm((DEC_BATCH, DEPTH, MEM_TOKENS, MEM_HEADS, MEM_HEAD_DIM), DEEPNORM_BETA)
    page_table = jax.random.permutation(next(ks), n_phys)[:DEC_BATCH * n_pages]
    page_table = page_table.reshape(DEC_BATCH, n_pages).astype(jnp.int32)

    col_scale = jnp.concatenate([
        jnp.ones((MLA_Q_LORA + MLA_KV_LORA + MLA_ROPE + 2 * DIFF_WIDTH,), f32),
        jnp.full((DIFF_WIDTH,), DEEPNORM_BETA, f32),
        jnp.ones((MEM_WIDTH + D_MIX,), f32)])
    w_in = nrm((DEPTH, D_MODEL, IN_COLS), D_MODEL ** -0.5) * col_scale
    q_norm = 1.0 + nrm((DEPTH, MLA_Q_LORA), 0.05)
    w_uq = nrm((DEPTH, MLA_Q_LORA, MLA_HEADS * (MLA_NOPE + MLA_ROPE)), MLA_Q_LORA ** -0.5)
    kv_norm = 1.0 + nrm((DEPTH, MLA_KV_LORA), 0.05)
    w_uk = nrm((DEPTH, MLA_KV_LORA, MLA_HEADS, MLA_NOPE), MLA_KV_LORA ** -0.5)
    w_uv = nrm((DEPTH, MLA_KV_LORA, MLA_HEADS, MLA_V), DEEPNORM_BETA * MLA_KV_LORA ** -0.5)
    lam_q1 = nrm((DEPTH, DIFF_HALF), 0.1)
    lam_k1 = nrm((DEPTH, DIFF_HALF), 0.1)
    lam_q2 = nrm((DEPTH, DIFF_HALF), 0.1)
    lam_k2 = nrm((DEPTH, DIFF_HALF), 0.1)
    diff_norm = 1.0 + nrm((DEPTH, DIFF_HEAD_DIM), 0.05)
    rel_bias = nrm((REL_BUCKETS, DIFF_HEADS), 0.3)
    w_mem_k = nrm((DEPTH, D_MODEL, MEM_WIDTH), D_MODEL ** -0.5)
    w_mem_v = nrm((DEPTH, D_MODEL, MEM_WIDTH), DEEPNORM_BETA * D_MODEL ** -0.5)
    w_out = nrm((DEPTH, D_MIX, D_MODEL), DEEPNORM_BETA * D_MIX ** -0.5)
    ln_g = 1.0 + nrm((DEPTH, D_MODEL), 0.05)
    ln_b = nrm((DEPTH, D_MODEL), 0.02)
    return {'x_prompt': x_prompt, 'x_sample': x_sample, 'mem_prompt': mem_prompt,
            'cache_mla_ckv': cache_mla_ckv, 'cache_mla_krope': cache_mla_krope,
            'cache_diff_k': cache_diff_k, 'cache_diff_v': cache_diff_v,
            'cache_mem_k': cache_mem_k, 'cache_mem_v': cache_mem_v, 'page_table': page_table,
            'w_in': w_in, 'q_norm': q_norm, 'w_uq': w_uq, 'kv_norm': kv_norm, 'w_uk': w_uk,
            'w_uv': w_uv, 'lam_q1': lam_q1, 'lam_k1': lam_k1, 'lam_q2': lam_q2, 'lam_k2': lam_k2,
            'diff_norm': diff_norm, 'rel_bias': rel_bias, 'w_mem_k': w_mem_k, 'w_mem_v': w_mem_v,
            'w_out': w_out, 'ln_g': ln_g, 'ln_b': ln_b}


def reference(x_prompt, x_sample, mem_prompt, cache_mla_ckv, cache_mla_krope, cache_diff_k,
              cache_diff_v, cache_mem_k, cache_mem_v, page_table, w_in, q_norm, w_uq, kv_norm,
              w_uk, w_uv, lam_q1, lam_k1, lam_q2, lam_k2, diff_norm, rel_bias, w_mem_k, w_mem_v,
              w_out, ln_g, ln_b):
    f32 = jnp.float32
    b_p, seq = x_prompt.shape[:2]
    dec_seq = x_sample.shape[1]
    past_len = page_table.shape[1] * cache_mla_ckv.shape[2]
    n_mem = mem_prompt.shape[1]
    pos_p = jnp.arange(seq, dtype=jnp.int32)
    q_pos_s = past_len + jnp.arange(dec_seq, dtype=jnp.int32)
    k_pos_s = jnp.arange(past_len + dec_seq, dtype=jnp.int32)

    h_p, h_s = x_prompt, x_sample
    new_p, new_s = [], []
    for l in range(DEPTH):
        lam_init = 0.8 - 0.6 * math.exp(-0.3 * l)
        lam = (jnp.exp(jnp.sum(lam_q1[l].astype(f32) * lam_k1[l].astype(f32)))
               - jnp.exp(jnp.sum(lam_q2[l].astype(f32) * lam_k2[l].astype(f32))) + lam_init)
        params = (w_in[l], q_norm[l], w_uq[l], kv_norm[l], w_uk[l], w_uv[l], diff_norm[l],
                  w_out[l], ln_g[l], ln_b[l])
        mem_k_p = (mem_prompt @ w_mem_k[l]).reshape(b_p, n_mem, MEM_HEADS, MEM_HEAD_DIM)
        mem_v_p = (mem_prompt @ w_mem_v[l]).reshape(b_p, n_mem, MEM_HEADS, MEM_HEAD_DIM)
        h_p, rows_p = decoder_layer(h_p, pos_p, pos_p, None, (mem_k_p, mem_v_p), lam, lam_init,
                                    sweep_query_blocks, params, rel_bias)
        new_p.append(rows_p + (mem_k_p, mem_v_p))
        past = (gather_pages(cache_mla_ckv, page_table, l), gather_pages(cache_mla_krope, page_table, l),
                gather_pages(cache_diff_k, page_table, l), gather_pages(cache_diff_v, page_table, l))
        h_s, rows_s = decoder_layer(h_s, q_pos_s, k_pos_s, past, (cache_mem_k[:, l], cache_mem_v[:, l]),
                                    lam, lam_init, all_queries, params, rel_bias)
        new_s.append(rows_s)

    p_ckv, p_kr, p_dk, p_dv, p_mk, p_mv = [jnp.stack(a, axis=1) for a in zip(*new_p)]
    s_ckv, s_kr, s_dk, s_dv = [jnp.stack(a, axis=1) for a in zip(*new_s)]
    return (h_p, h_s, p_ckv, p_kr, p_dk, p_dv, p_mk, p_mv, s_ckv, s_kr, s_dk, s_dv)
```

```python
import functools
import math

import jax
import jax.numpy as jnp
import numpy as np
from jax import lax
from jax.experimental import pallas as pl
from jax.experimental.pallas import tpu as pltpu

F32 = jnp.float32
BF16 = jnp.bfloat16

D_MODEL = 1024
DEPTH = 2
PAGE = 128
MLA_HEADS = 4
MLA_NOPE = 128
MLA_ROPE = 64
Q_LORA = 384
KV_LORA = 256
DIFF_HEADS = 4
DIFF_HALF = 32
DIFF_HEAD_DIM = 64
DIFF_WIDTH = 256
MEM_HEADS = 4
MEM_HEAD_DIM = 64
MEM_WIDTH = 256
D_MIX = 1024
REL_BUCKETS = 32
REL_MAX_DIST = 128
ROPE_BASE = 10000.0
MLA_SCALE = (MLA_NOPE + MLA_ROPE) ** -0.5
DIFF_SCALE = DIFF_HALF ** -0.5
MEM_SCALE = MEM_HEAD_DIM ** -0.5
RMS_EPS = 1e-6
LN_EPS = 1e-5
DEEPNORM_ALPHA = (2 * DEPTH) ** 0.25

LANES = 128
NEG = -1e30
TQ = 256
TM = 256
PPC = 16
VMEM_LIMIT = 48 * 1024 * 1024
N_GRP = 2 * DIFF_HEADS

_SEG = {}
_off = 0
for _name, _w in (("cq", Q_LORA), ("ckv", KV_LORA), ("xk", LANES), ("yk", LANES), ("dq", DIFF_WIDTH),
                  ("mq", MEM_WIDTH), ("gate", D_MIX)):
    _SEG[_name] = (_off, _off + _w)
    _off += _w
IN_ROW = _off
_SEGT = {"dk": (0, 256), "dv": (256, 512), "xk": (512, 640), "yk": (640, 768)}
IN_T = 768


def _nt(a, b):
    return lax.dot_general(a, b, (((1,), (1,)), ((), ())), preferred_element_type=F32)


def _nn(a, b):
    return jnp.dot(a, b, preferred_element_type=F32)


def _rep(x, n):
    if n <= LANES:
        return x[:, :n]
    return jnp.concatenate([x] * (n // LANES), axis=1)


def _online_update(s, v, m_ref, l_ref, acc_ref, v_feature_major=False):
    m_prev = m_ref[...]
    m_new = jnp.maximum(m_prev, jnp.max(s, axis=1, keepdims=True))
    alpha = jnp.exp(m_prev - m_new)
    p = jnp.exp(s - _rep(m_new, s.shape[1]))
    l_ref[...] = alpha * l_ref[...] + jnp.sum(p, axis=1, keepdims=True)
    m_ref[...] = m_new
    pb = p.astype(BF16)
    pv = _nt(pb, v) if v_feature_major else _nn(pb, v)
    acc_ref[...] = acc_ref[...] * _rep(alpha, pv.shape[1]) + pv


def _init_softmax_state(m_ref, l_ref, acc_ref):
    m_ref[...] = jnp.full(m_ref.shape, NEG, F32)
    l_ref[...] = jnp.zeros(l_ref.shape, F32)
    acc_ref[...] = jnp.zeros(acc_ref.shape, F32)


def _lam_from(lamv_ref, lam_init):
    a = jnp.sum(lamv_ref[0:1, :] * lamv_ref[1:2, :], axis=1, keepdims=True)
    b = jnp.sum(lamv_ref[2:3, :] * lamv_ref[3:4, :], axis=1, keepdims=True)
    return jnp.exp(a) - jnp.exp(b) + lam_init


def _rms(x, g):
    return x * lax.rsqrt(jnp.mean(x * x, axis=-1, keepdims=True) + RMS_EPS) * g


def _bias_kernel(tab_ref, b_ref, o_ref):
    b = b_ref[...]
    for h in range(DIFF_HEADS):
        acc = jnp.full(b.shape, NEG, F32)
        for k in range(REL_BUCKETS):
            acc = jnp.where(b == k, tab_ref[k, h], acc)
        o_ref[h] = acc


def _bias_lookup(bucket, rel_bias, tn):
    n, m = bucket.shape
    return pl.pallas_call(
        _bias_kernel,
        grid=(n // tn,),
        in_specs=[pl.BlockSpec(memory_space=pltpu.SMEM),
                  pl.BlockSpec((tn, m), lambda i: (i, 0))],
        out_specs=pl.BlockSpec((DIFF_HEADS, tn, m), lambda i: (0, i, 0)),
        out_shape=jax.ShapeDtypeStruct((DIFF_HEADS, n, m), F32),
        name="rel_bias_lookup",
    )(rel_bias, bucket)


def _rel_bucket(n):
    max_exact = REL_BUCKETS // 2
    nf = jnp.maximum(n, 1).astype(F32)
    large = max_exact + (jnp.log(nf / max_exact) / math.log(REL_MAX_DIST / max_exact)
                         * (REL_BUCKETS - max_exact)).astype(jnp.int32)
    large = jnp.minimum(large, REL_BUCKETS - 1)
    return jnp.where(n < max_exact, n, large)


def _proj_kernel(h_ref, cos_ref, sin_ref, cost_ref, sint_ref, w_in_ref, w_int_ref, qn_ref, w_uq_ref, kvn_ref,
                 w_ukt_ref, qlat_ref, qrope_ref, ckv_ref, krt_ref, katt_ref, dq_ref, dkt_ref, dvt_ref,
                 dktb_ref, dvtb_ref, mq_ref, sg_ref):
    hb = h_ref[...].astype(BF16)

    def seg(name):
        lo, hi = _SEG[name]
        return _nn(hb, w_in_ref[:, lo:hi])

    def seg_t(name):
        lo, hi = _SEGT[name]
        return _nt(w_int_ref[lo:hi, :], hb)

    cos = cos_ref[...]
    sin = sin_ref[...]

    cq = _rms(seg("cq"), qn_ref[...])
    q = _nn(cq.astype(BF16), w_uq_ref[...])
    nope_w = MLA_HEADS * MLA_NOPE
    for pair in range(MLA_HEADS // 2):
        x = q[:, nope_w + pair * LANES: nope_w + (pair + 1) * LANES]
        y = q[:, nope_w + (2 + pair) * LANES: nope_w + (3 + pair) * LANES]
        r = x * cos + y * sin
        qrope_ref[:, pair * LANES:(pair + 1) * LANES] = (r * MLA_SCALE).astype(BF16)
    for h in range(MLA_HEADS):
        nope = q[:, h * MLA_NOPE:(h + 1) * MLA_NOPE].astype(BF16)
        qlat_ref[h] = (_nn(nope, w_ukt_ref[h]) * MLA_SCALE).astype(BF16)

    ckv = _rms(seg("ckv"), kvn_ref[...])
    ckv_ref[...] = ckv
    rk = seg("xk") * cos + seg("yk") * sin
    katt_ref[:, 0:KV_LORA] = ckv.astype(BF16)
    katt_ref[:, KV_LORA:KV_LORA + LANES] = rk.astype(BF16)
    rkt = seg_t("xk") * cost_ref[...] + seg_t("yk") * sint_ref[...]
    krt_ref[...] = rkt[:MLA_ROPE, :]

    dq_ref[...] = (seg("dq") * DIFF_SCALE).astype(BF16)
    dkt = seg_t("dk")
    dkt_ref[...] = dkt
    dktb_ref[0] = dkt.astype(BF16)
    dvt = seg_t("dv")
    dvt_ref[...] = dvt
    dvtb_ref[0] = dvt.astype(BF16)
    mq_ref[...] = (seg("mq") * MEM_SCALE).astype(BF16)
    gate = seg("gate")
    sg_ref[...] = gate * (1.0 / (1.0 + jnp.exp(-gate)))


def _proj(h, groups, tabs, w_in, w_int, qn, w_uq, kvn, w_ukt):
    rows = h.shape[0]
    tm = TM
    rpg = rows // groups
    gb = rpg // tm
    cos_t, sin_t, cos_tt, sin_tt = tabs
    period_blocks = cos_t.shape[0] // tm
    row = lambda w: pl.BlockSpec((tm, w), lambda i: (i, 0))
    full2 = lambda a: pl.BlockSpec(a.shape, lambda i: (0, 0))
    full3 = lambda a: pl.BlockSpec(a.shape, lambda i: (0, 0, 0))
    tab = pl.BlockSpec((tm, LANES), lambda i: (i % period_blocks, 0))
    tab_t = pl.BlockSpec((LANES, tm), lambda i: (0, i % period_blocks))
    fmaj = lambda f: pl.BlockSpec((None, f, tm), lambda i: (i // gb, 0, i % gb))
    chunked = pl.BlockSpec((1, DIFF_WIDTH, tm), lambda i: (i, 0, 0))
    out_shape = (
        jax.ShapeDtypeStruct((MLA_HEADS, rows, KV_LORA), BF16),
        jax.ShapeDtypeStruct((rows, 2 * LANES), BF16),
        jax.ShapeDtypeStruct((rows, KV_LORA), F32),
        jax.ShapeDtypeStruct((groups, MLA_ROPE, rpg), F32),
        jax.ShapeDtypeStruct((rows, KV_LORA + LANES), BF16),
        jax.ShapeDtypeStruct((rows, DIFF_WIDTH), BF16),
        jax.ShapeDtypeStruct((groups, DIFF_WIDTH, rpg), F32),
        jax.ShapeDtypeStruct((groups, DIFF_WIDTH, rpg), F32),
        jax.ShapeDtypeStruct((rows // tm, DIFF_WIDTH, tm), BF16),
        jax.ShapeDtypeStruct((rows // tm, DIFF_WIDTH, tm), BF16),
        jax.ShapeDtypeStruct((rows, MEM_WIDTH), BF16),
        jax.ShapeDtypeStruct((rows, D_MIX), F32),
    )
    out_specs = (
        pl.BlockSpec((MLA_HEADS, tm, KV_LORA), lambda i: (0, i, 0)),
        row(2 * LANES), row(KV_LORA), fmaj(MLA_ROPE), row(KV_LORA + LANES), row(DIFF_WIDTH),
        fmaj(DIFF_WIDTH), fmaj(DIFF_WIDTH), chunked, chunked, row(MEM_WIDTH), row(D_MIX),
    )
    return pl.pallas_call(
        _proj_kernel,
        grid=(rows // tm,),
        in_specs=[row(D_MODEL), tab, tab, tab_t, tab_t, full2(w_in), full2(w_int), full2(qn), full2(w_uq),
                  full2(kvn), full3(w_ukt)],
        out_specs=out_specs,
        out_shape=out_shape,
        compiler_params=pltpu.CompilerParams(dimension_semantics=("arbitrary",),
                                             vmem_limit_bytes=VMEM_LIMIT),
        name="proj",
    )(h, cos_t, sin_t, cos_tt, sin_tt, w_in, w_int, qn, w_uq, kvn, w_ukt)


def _memkv_kernel(x_ref, wt_ref, o_ref):
    o_ref[...] = _nt(wt_ref[...], x_ref[...].astype(BF16))


def _memkv(x, wt):
    b, n_mem, _ = x.shape
    n = wt.shape[0]
    return pl.pallas_call(
        _memkv_kernel,
        grid=(b,),
        in_specs=[pl.BlockSpec((None, n_mem, D_MODEL), lambda i: (i, 0, 0)), pl.BlockSpec(wt.shape, lambda i: (0, 0))],
        out_specs=pl.BlockSpec((None, n, n_mem), lambda i: (i, 0, 0)),
        out_shape=jax.ShapeDtypeStruct((b, n, n_mem), F32),
        name="memkv_proj",
    )(x, wt)


def _mla_prompt_kernel(qlat_ref, qrope_ref, katt_ref, o_ref, qcat, m_scr, l_scr, acc_scr):
    i = pl.program_id(1)
    tq = qrope_ref.shape[0]
    lane = lax.broadcasted_iota(jnp.int32, (tq, LANES), 1)
    for h in range(MLA_HEADS):
        qcat[h, :, 0:KV_LORA] = qlat_ref[h]
        t = qrope_ref[:, (h // 2) * LANES:(h // 2 + 1) * LANES]
        qcat[h, :, KV_LORA:KV_LORA + LANES] = jnp.where((lane // MLA_ROPE) == (h % 2), t, jnp.zeros_like(t))
    _init_softmax_state(m_scr, l_scr, acc_scr)

    def step(j, masked):
        k = katt_ref[pl.ds(pl.multiple_of(j * tq, tq), tq), :]
        v = k[:, 0:KV_LORA]
        for h in range(MLA_HEADS):
            s = _nt(qcat[h], k)
            if masked:
                r = lax.broadcasted_iota(jnp.int32, s.shape, 0)
                c = lax.broadcasted_iota(jnp.int32, s.shape, 1)
                s = jnp.where(c <= r, s, NEG)
            _online_update(s, v, m_scr.at[h], l_scr.at[h], acc_scr.at[h])

    def body(j, carry):
        step(j, False)
        return carry

    lax.fori_loop(0, i, body, 0)
    step(i, True)
    for h in range(MLA_HEADS):
        o_ref[h] = (acc_scr[h] / _rep(l_scr[h], KV_LORA)).astype(BF16)


def _mla_prompt(qlat, qrope, katt, batch, seq):
    rows = batch * seq
    nq = seq // TQ
    kw = KV_LORA + LANES
    return pl.pallas_call(
        _mla_prompt_kernel,
        grid=(batch, nq),
        in_specs=[pl.BlockSpec((MLA_HEADS, TQ, KV_LORA), lambda b, i: (0, b * nq + i, 0)),
                  pl.BlockSpec((TQ, 2 * LANES), lambda b, i: (b * nq + i, 0)),
                  pl.BlockSpec((seq, kw), lambda b, i: (b, 0))],
        out_specs=pl.BlockSpec((MLA_HEADS, TQ, KV_LORA), lambda b, i: (0, b * nq + i, 0)),
        out_shape=jax.ShapeDtypeStruct((MLA_HEADS, rows, KV_LORA), BF16),
        scratch_shapes=[pltpu.VMEM((MLA_HEADS, TQ, kw), BF16),
                        pltpu.VMEM((MLA_HEADS, TQ, LANES), F32),
                        pltpu.VMEM((MLA_HEADS, TQ, LANES), F32),
                        pltpu.VMEM((MLA_HEADS, TQ, KV_LORA), F32)],
        compiler_params=pltpu.CompilerParams(dimension_semantics=("arbitrary", "arbitrary"),
                                             vmem_limit_bytes=VMEM_LIMIT),
        name="mla_prompt",
    )(qlat, qrope, katt)


def _diff_combine(lam, o, rows):
    lane = lax.broadcasted_iota(jnp.int32, (rows, DIFF_WIDTH), 1)
    out = jnp.zeros((rows, DIFF_WIDTH), F32)
    for h in range(DIFF_HEADS):
        o1 = o[(2 * h) * rows:(2 * h + 1) * rows]
        o2 = o[(2 * h + 1) * rows:(2 * h + 2) * rows]
        out = jnp.where((lane // DIFF_HEAD_DIM) == h, o1 - lam * o2, out)
    return out


def _diff_prompt_kernel(lamv_ref, dq_ref, dk_ref, dv_ref, bias_ref, o_ref, qd, m_scr, l_scr, acc_scr,
                        *, lam_init, n_bias):
    i = pl.program_id(1)
    tq = dq_ref.shape[0]
    lane = lax.broadcasted_iota(jnp.int32, (tq, DIFF_WIDTH), 1)
    dq = dq_ref[...]
    for g in range(N_GRP):
        qd[g * tq:(g + 1) * tq, :] = jnp.where((lane // DIFF_HALF) == g, dq, jnp.zeros_like(dq))
    _init_softmax_state(m_scr, l_scr, acc_scr)

    def step(j, dd):
        k = dk_ref[j]
        v = dv_ref[j]
        for g in range(N_GRP):
            rs = slice(g * tq, (g + 1) * tq)
            s = _nn(qd[rs, :], k) + bias_ref[(g // 2) * n_bias + dd]
            _online_update(s, v, m_scr.at[rs], l_scr.at[rs], acc_scr.at[rs], v_feature_major=True)

    def body(j, carry):
        step(j, jnp.minimum(i - j, n_bias - 1))
        return carry

    lax.fori_loop(0, i, body, 0)
    step(i, 0)
    o = acc_scr[...] / _rep(l_scr[...], DIFF_WIDTH)
    o_ref[...] = _diff_combine(_lam_from(lamv_ref, lam_init), o, tq)


def _diff_prompt(lamv, dq, dktb, dvtb, bias, batch, seq, lam_init, n_bias):
    rows = batch * seq
    nq = seq // TQ
    kv_spec = pl.BlockSpec((nq, DIFF_WIDTH, TQ), lambda b, i: (b, 0, 0))
    return pl.pallas_call(
        functools.partial(_diff_prompt_kernel, lam_init=lam_init, n_bias=n_bias),
        grid=(batch, nq),
        in_specs=[pl.BlockSpec(lamv.shape, lambda b, i: (0, 0)),
                  pl.BlockSpec((TQ, DIFF_WIDTH), lambda b, i: (b * nq + i, 0)),
                  kv_spec, kv_spec,
                  pl.BlockSpec(bias.shape, lambda b, i: (0, 0, 0))],
        out_specs=pl.BlockSpec((TQ, DIFF_WIDTH), lambda b, i: (b * nq + i, 0)),
        out_shape=jax.ShapeDtypeStruct((rows, DIFF_WIDTH), F32),
        scratch_shapes=[pltpu.VMEM((N_GRP * TQ, DIFF_WIDTH), BF16),
                        pltpu.VMEM((N_GRP * TQ, LANES), F32),
                        pltpu.VMEM((N_GRP * TQ, LANES), F32),
                        pltpu.VMEM((N_GRP * TQ, DIFF_WIDTH), F32)],
        compiler_params=pltpu.CompilerParams(dimension_semantics=("arbitrary", "arbitrary"),
                                             vmem_limit_bytes=VMEM_LIMIT),
        name="diff_prompt",
    )(lamv, dq, dktb, dvtb, bias)


def _mem_kernel(mq_ref, mk_ref, mv_ref, o_ref):
    mq = mq_ref[...]
    k = mk_ref[...].astype(BF16)
    v = mv_ref[...].astype(BF16)
    lane = lax.broadcasted_iota(jnp.int32, mq.shape, 1)
    out = jnp.zeros(mq.shape, F32)
    for h in range(MEM_HEADS):
        sel = (lane // MEM_HEAD_DIM) == h
        s = _nn(jnp.where(sel, mq, jnp.zeros_like(mq)), k)
        p = jnp.exp(s - jnp.max(s, axis=1, keepdims=True))
        l = jnp.sum(p, axis=1, keepdims=True)
        pv = _nt(p.astype(BF16), v)
        out = jnp.where(sel, pv / l, out)
    o_ref[...] = out


def _mem_attn(mq, mk, mv, layer, tr):
    g, r, _ = mq.shape
    n_mem = mk.shape[3]
    kv_spec = pl.BlockSpec((None, None, MEM_WIDTH, n_mem), lambda a, b: (a, layer, 0, 0))
    return pl.pallas_call(
        _mem_kernel,
        grid=(g, r // tr),
        in_specs=[pl.BlockSpec((None, tr, MEM_WIDTH), lambda a, b: (a, b, 0)), kv_spec, kv_spec],
        out_specs=pl.BlockSpec((None, tr, MEM_WIDTH), lambda a, b: (a, b, 0)),
        out_shape=jax.ShapeDtypeStruct((g, r, MEM_WIDTH), F32),
        name="mem_attn",
    )(mq, mk, mv)


def _out_kernel(olat_ref, odiff_ref, omem_ref, sg_ref, h_ref, w_uv_ref, dn_ref, w_out_ref, g_ref, b_ref,
                y_ref, *, diff_gain):
    parts = []
    for h in range(MLA_HEADS):
        parts.append(_nn(olat_ref[h], w_uv_ref[h]))
    od = odiff_ref[...]
    lane = lax.broadcasted_iota(jnp.int32, od.shape, 1)
    sq = od * od
    inv = jnp.zeros(od.shape, F32)
    for h in range(DIFF_HEADS):
        sel = (lane // DIFF_HEAD_DIM) == h
        ms = jnp.sum(jnp.where(sel, sq, 0.0), axis=1, keepdims=True) * (1.0 / DIFF_HEAD_DIM)
        inv = jnp.where(sel, lax.rsqrt(ms + RMS_EPS), inv)
    parts.append(od * inv * dn_ref[...] * diff_gain)
    parts.append(omem_ref[...])
    branch = jnp.concatenate(parts, axis=1) * sg_ref[...]
    y = _nn(branch.astype(BF16), w_out_ref[...])
    x = DEEPNORM_ALPHA * h_ref[...] + y
    mu = jnp.mean(x, axis=-1, keepdims=True)
    xc = x - mu
    var = jnp.mean(xc * xc, axis=-1, keepdims=True)
    y_ref[...] = xc * lax.rsqrt(var + LN_EPS) * g_ref[...] + b_ref[...]


def _out_proj(olat, odiff, omem, sg, h, w_uv, dn, w_out, ln_g, ln_b, diff_gain):
    rows = h.shape[0]
    tm = TM
    row = lambda w: pl.BlockSpec((tm, w), lambda i: (i, 0))
    full2 = lambda a: pl.BlockSpec(a.shape, lambda i: (0, 0))
    return pl.pallas_call(
        functools.partial(_out_kernel, diff_gain=diff_gain),
        grid=(rows // tm,),
        in_specs=[pl.BlockSpec((MLA_HEADS, tm, KV_LORA), lambda i: (0, i, 0)),
                  row(DIFF_WIDTH), row(MEM_WIDTH), row(D_MIX), row(D_MODEL),
                  pl.BlockSpec(w_uv.shape, lambda i: (0, 0, 0)), full2(dn), full2(w_out), full2(ln_g),
                  full2(ln_b)],
        out_specs=row(D_MODEL),
        out_shape=jax.ShapeDtypeStruct((rows, D_MODEL), F32),
        compiler_params=pltpu.CompilerParams(dimension_semantics=("arbitrary",),
                                             vmem_limit_bytes=VMEM_LIMIT),
        name="out_proj",
    )(olat, odiff, omem, sg, h, w_uv, dn, w_out, ln_g, ln_b)


def _page_copies(pt_ref, b, c, slot, layer, caches, bufs, feature_major, sems):
    out = []
    for p in range(PPC):
        pid = pt_ref[b, c * PPC + p]
        for a, (cache, buf, fm) in enumerate(zip(caches, bufs, feature_major)):
            if fm:
                dst = buf.at[slot, :, pl.ds(p * PAGE, PAGE)]
            else:
                dst = buf.at[slot, pl.ds(p * PAGE, PAGE), :]
            out.append(pltpu.make_async_copy(cache.at[pid, layer], dst, sems.at[a, slot]))
    return out


def _paged_pipeline(pt_ref, layer, caches, bufs, feature_major, sems, n_chunks):
    b = pl.program_id(0)
    c = pl.program_id(1)
    step = b * n_chunks + c
    total = pl.num_programs(0) * n_chunks
    slot = step % 2
    args = (layer, caches, bufs, feature_major, sems)

    @pl.when(step == 0)
    def _():
        for cp in _page_copies(pt_ref, b, c, slot, *args):
            cp.start()

    @pl.when(step + 1 < total)
    def _():
        nxt = step + 1
        for cp in _page_copies(pt_ref, nxt // n_chunks, nxt % n_chunks, 1 - slot, *args):
            cp.start()

    for cp in _page_copies(pt_ref, b, c, slot, *args):
        cp.wait()
    return slot


def _mla_sample_kernel(pt_ref, q_ref, qr_ref, ckvn_ref, krn_ref, cache_ckv, cache_krt, o_ref,
                       ckv_buf, krt_buf, sems, m_scr, l_scr, acc_scr, *, layer, n_chunks):
    c = pl.program_id(1)
    slot = _paged_pipeline(pt_ref, layer, (cache_ckv, cache_krt), (ckv_buf, krt_buf), (False, True), sems,
                           n_chunks)

    @pl.when(c == 0)
    def _():
        _init_softmax_state(m_scr, l_scr, acc_scr)

    q = q_ref[...]
    qr = qr_ref[...]
    k = ckv_buf[slot].astype(BF16)
    krt = krt_buf[slot].astype(BF16)
    _online_update(_nt(q, k) + _nn(qr, krt), k, m_scr, l_scr, acc_scr)

    @pl.when(c == n_chunks - 1)
    def _():
        kn = ckvn_ref[...].astype(BF16)
        s = _nt(q, kn) + _nt(qr, krn_ref[...].astype(BF16))
        t_q = lax.broadcasted_iota(jnp.int32, s.shape, 0) % s.shape[1]
        t_k = lax.broadcasted_iota(jnp.int32, s.shape, 1)
        _online_update(jnp.where(t_k <= t_q, s, NEG), kn, m_scr, l_scr, acc_scr)
        o_ref[...] = (acc_scr[...] / _rep(l_scr[...], KV_LORA)).astype(BF16)


def _mla_sample(page_table, q, qr, ckv_new, kr_new, cache_ckv, cache_krt, layer):
    nb, n_pages = page_table.shape
    n_chunks = n_pages // PPC
    rows = q.shape[1]
    dec = ckv_new.shape[1]
    per_b = lambda r, w: pl.BlockSpec((None, r, w), lambda b, c, pt: (b, 0, 0))
    grid_spec = pltpu.PrefetchScalarGridSpec(
        num_scalar_prefetch=1,
        grid=(nb, n_chunks),
        in_specs=[per_b(rows, KV_LORA), per_b(rows, MLA_ROPE), per_b(dec, KV_LORA), per_b(dec, MLA_ROPE),
                  pl.BlockSpec(memory_space=pl.ANY), pl.BlockSpec(memory_space=pl.ANY)],
        out_specs=per_b(rows, KV_LORA),
        scratch_shapes=[pltpu.VMEM((2, PPC * PAGE, KV_LORA), F32),
                        pltpu.VMEM((2, MLA_ROPE, PPC * PAGE), F32),
                        pltpu.SemaphoreType.DMA((2, 2)),
                        pltpu.VMEM((rows, LANES), F32),
                        pltpu.VMEM((rows, LANES), F32),
                        pltpu.VMEM((rows, KV_LORA), F32)])
    return pl.pallas_call(
        functools.partial(_mla_sample_kernel, layer=layer, n_chunks=n_chunks),
        grid_spec=grid_spec,
        out_shape=jax.ShapeDtypeStruct((nb, rows, KV_LORA), BF16),
        compiler_params=pltpu.CompilerParams(dimension_semantics=("arbitrary", "arbitrary"),
                                             vmem_limit_bytes=VMEM_LIMIT),
        name="mla_sample",
    )(page_table, q, qr, ckv_new, kr_new, cache_ckv, cache_krt)


def _diff_sample_kernel(pt_ref, lamv_ref, qd_ref, dkn_ref, dvn_ref, bias_ref, biasn_ref, cache_dkt, cache_dvt,
                        o_ref, dk_buf, dv_buf, sems, m_scr, l_scr, acc_scr, *, layer, n_chunks, lam_init):
    c = pl.program_id(1)
    slot = _paged_pipeline(pt_ref, layer, (cache_dkt, cache_dvt), (dk_buf, dv_buf), (True, True), sems,
                           n_chunks)

    @pl.when(c == 0)
    def _():
        _init_softmax_state(m_scr, l_scr, acc_scr)

    qd = qd_ref[...]
    k = dk_buf[slot].astype(BF16)
    v = dv_buf[slot].astype(BF16)
    _online_update(_nn(qd, k) + bias_ref[c], v, m_scr, l_scr, acc_scr, v_feature_major=True)

    @pl.when(c == n_chunks - 1)
    def _():
        dec = dkn_ref.shape[0]
        s = _nt(qd, dkn_ref[...].astype(BF16)) + biasn_ref[:, :dec]
        _online_update(s, dvn_ref[...].astype(BF16), m_scr, l_scr, acc_scr)
        o = acc_scr[...] / _rep(l_scr[...], DIFF_WIDTH)
        o_ref[...] = _diff_combine(_lam_from(lamv_ref, lam_init), o, dec)


def _diff_sample(page_table, lamv, qd, dk_new, dv_new, bias, bias_new, cache_dkt, cache_dvt, layer, lam_init):
    nb, n_pages = page_table.shape
    n_chunks = n_pages // PPC
    rows = qd.shape[1]
    dec = dk_new.shape[1]
    per_b = lambda r, w: pl.BlockSpec((None, r, w), lambda b, c, pt: (b, 0, 0))
    grid_spec = pltpu.PrefetchScalarGridSpec(
        num_scalar_prefetch=1,
        grid=(nb, n_chunks),
        in_specs=[pl.BlockSpec(lamv.shape, lambda b, c, pt: (0, 0)),
                  per_b(rows, DIFF_WIDTH), per_b(dec, DIFF_WIDTH), per_b(dec, DIFF_WIDTH),
                  pl.BlockSpec(bias.shape, lambda b, c, pt: (0, 0, 0)),
                  pl.BlockSpec(bias_new.shape, lambda b, c, pt: (0, 0)),
                  pl.BlockSpec(memory_space=pl.ANY), pl.BlockSpec(memory_space=pl.ANY)],
        out_specs=per_b(dec, DIFF_WIDTH),
        scratch_shapes=[pltpu.VMEM((2, DIFF_WIDTH, PPC * PAGE), F32),
                        pltpu.VMEM((2, DIFF_WIDTH, PPC * PAGE), F32),
                        pltpu.SemaphoreType.DMA((2, 2)),
                        pltpu.VMEM((rows, LANES), F32),
                        pltpu.VMEM((rows, LANES), F32),
                        pltpu.VMEM((rows, DIFF_WIDTH), F32)])
    return pl.pallas_call(
        functools.partial(_diff_sample_kernel, layer=layer, n_chunks=n_chunks, lam_init=lam_init),
        grid_spec=grid_spec,
        out_shape=jax.ShapeDtypeStruct((nb, dec, DIFF_WIDTH), F32),
        compiler_params=pltpu.CompilerParams(dimension_semantics=("arbitrary", "arbitrary"),
                                             vmem_limit_bytes=VMEM_LIMIT),
        name="diff_sample",
    )(page_table, lamv, qd, dk_new, dv_new, bias, bias_new, cache_dkt, cache_dvt)


def _rope_tables(pos, reps):
    half = MLA_ROPE // 2
    inv = ROPE_BASE ** (-jnp.arange(half, dtype=F32) * (2.0 / MLA_ROPE))
    ang = pos.astype(F32)[:, None] * inv[None, :]
    cos, sin = jnp.cos(ang), jnp.sin(ang)
    cos_t = jnp.tile(jnp.concatenate([cos, cos, cos, cos], axis=1), (reps, 1))
    sin_t = jnp.tile(jnp.concatenate([-sin, sin, -sin, sin], axis=1), (reps, 1))
    return cos_t, sin_t, cos_t.T, sin_t.T


def _w_in_cols():
    k1 = 640 + np.arange(32)
    k2 = 672 + np.arange(32)
    xk = np.concatenate([k1, k2, k1, k2])
    yk = np.concatenate([k2, k1, k2, k1])
    dq, dk, dv, mq, gate = (np.arange(a, b) for a, b in ((704, 960), (960, 1216), (1216, 1472), (1472, 1728),
                                                           (1728, 2752)))
    return (np.concatenate([np.arange(0, 640), xk, yk, dq, mq, gate]), np.concatenate([dk, dv, xk, yk]))


def _w_uq_cols():
    hd = MLA_NOPE + MLA_ROPE
    half = MLA_ROPE // 2
    nope = [h * hd + np.arange(MLA_NOPE) for h in range(MLA_HEADS)]
    x1 = [h * hd + MLA_NOPE + np.arange(half) for h in range(MLA_HEADS)]
    x2 = [h * hd + MLA_NOPE + half + np.arange(half) for h in range(MLA_HEADS)]
    xs = [np.concatenate([x1[2 * p], x2[2 * p], x1[2 * p + 1], x2[2 * p + 1]]) for p in range(2)]
    ys = [np.concatenate([x2[2 * p], x1[2 * p], x2[2 * p + 1], x1[2 * p + 1]]) for p in range(2)]
    return np.concatenate(nope + xs + ys)


def kernel(x_prompt, x_sample, mem_prompt, cache_mla_ckv, cache_mla_krope, cache_diff_k, cache_diff_v,
           cache_mem_k, cache_mem_v, page_table, w_in, q_norm, w_uq, kv_norm, w_uk, w_uv, lam_q1, lam_k1,
           lam_q2, lam_k2, diff_norm, rel_bias, w_mem_k, w_mem_v, w_out, ln_g, ln_b):
    b_p, seq, _ = x_prompt.shape
    nb, dec, _ = x_sample.shape
    n_phys = cache_mla_ckv.shape[0]
    n_pages = page_table.shape[1]
    past_len = n_pages * PAGE
    n_mem = mem_prompt.shape[1]
    n_chunks = n_pages // PPC
    chunk = PPC * PAGE
    rows_s = nb * dec

    tabs_p = _rope_tables(jnp.arange(seq, dtype=jnp.int32), 1)
    tabs_s = _rope_tables(past_len + jnp.arange(dec, dtype=jnp.int32), TM // dec)

    n_bias = 1 + -(-(REL_MAX_DIST - 1) // TQ) + 1
    r = jnp.arange(TQ, dtype=jnp.int32)
    n_p = (jnp.arange(n_bias, dtype=jnp.int32)[:, None, None] * TQ + r[None, :, None] - r[None, None, :])
    bucket_p = jnp.where(n_p >= 0, _rel_bucket(jnp.maximum(n_p, 0)), -1).reshape(n_bias * TQ, TQ)
    bias_p = _bias_lookup(bucket_p, rel_bias, TQ).reshape(DIFF_HEADS * n_bias, TQ, TQ)
    k_len = past_len + LANES
    n_s = (past_len + jnp.arange(dec, dtype=jnp.int32))[:, None] - jnp.arange(k_len, dtype=jnp.int32)[None, :]
    valid_s = (n_s >= 0) & (jnp.arange(k_len)[None, :] < past_len + dec)
    bucket_s = jnp.where(valid_s, _rel_bucket(jnp.maximum(n_s, 0)), -1)
    bias_s = _bias_lookup(bucket_s, rel_bias, dec)
    bias_s = jnp.broadcast_to(bias_s[:, None], (DIFF_HEADS, 2, dec, k_len)).reshape(N_GRP * dec, k_len)
    bias_s_past = bias_s[:, :past_len].reshape(N_GRP * dec, n_chunks, chunk).transpose(1, 0, 2)
    bias_s_new = bias_s[:, past_len:]

    cache_krt = jnp.transpose(cache_mla_krope, (0, 1, 3, 2))
    cache_dkt = jnp.transpose(cache_diff_k, (0, 1, 3, 4, 2)).reshape(n_phys, DEPTH, DIFF_WIDTH, PAGE)
    cache_dvt = jnp.transpose(cache_diff_v, (0, 1, 3, 4, 2)).reshape(n_phys, DEPTH, DIFF_WIDTH, PAGE)
    cache_mkt = jnp.transpose(cache_mem_k, (0, 1, 3, 4, 2)).reshape(nb, DEPTH, MEM_WIDTH, n_mem)
    cache_mvt = jnp.transpose(cache_mem_v, (0, 1, 3, 4, 2)).reshape(nb, DEPTH, MEM_WIDTH, n_mem)

    cols_row, cols_t = _w_in_cols()
    w_uq_cols = _w_uq_cols()
    grp_lane = jnp.arange(DIFF_WIDTH, dtype=jnp.int32) // DIFF_HALF

    h_p = x_prompt.reshape(b_p * seq, D_MODEL)
    h_s = x_sample.reshape(rows_s, D_MODEL)
    new_p, new_s = [], []
    for l in range(DEPTH):
        lam_init = 0.8 - 0.6 * math.exp(-0.3 * l)
        lamv = jnp.stack([lam_q1[l], lam_k1[l], lam_q2[l], lam_k2[l]]).astype(F32)
        w_in_t = w_in[l].T
        w_in_l = w_in_t[cols_row].T.astype(BF16)
        w_int_l = w_in_t[cols_t].astype(BF16)
        w_uq_l = w_uq[l][:, w_uq_cols].astype(BF16)
        w_ukt_l = jnp.transpose(w_uk[l], (1, 2, 0)).astype(BF16)
        w_uv_l = jnp.transpose(w_uv[l], (1, 0, 2)).astype(BF16)
        w_out_l = w_out[l].astype(BF16)
        qn_l = q_norm[l].reshape(1, Q_LORA)
        kvn_l = kv_norm[l].reshape(1, KV_LORA)
        dn_l = jnp.tile(diff_norm[l], DIFF_HEADS).reshape(1, DIFF_WIDTH)
        g_l = ln_g[l].reshape(1, D_MODEL)
        bb_l = ln_b[l].reshape(1, D_MODEL)
        w_memt_l = jnp.concatenate([w_mem_k[l].T, w_mem_v[l].T], axis=0).astype(BF16)
        diff_gain = 1.0 - lam_init
        proj_w = (w_in_l, w_int_l, qn_l, w_uq_l, kvn_l, w_ukt_l)

        memkvt = _memkv(mem_prompt, w_memt_l)
        mem_kt = memkvt[:, :MEM_WIDTH].reshape(b_p, 1, MEM_WIDTH, n_mem)
        mem_vt = memkvt[:, MEM_WIDTH:].reshape(b_p, 1, MEM_WIDTH, n_mem)
        (qlat, qrope, ckv, krt, katt, dq, dkt, dvt, dktb, dvtb, mq, sg) = _proj(h_p, b_p, tabs_p, *proj_w)
        olat = _mla_prompt(qlat, qrope, katt, b_p, seq)
        odiff = _diff_prompt(lamv, dq, dktb, dvtb, bias_p, b_p, seq, lam_init, n_bias)
        omem = _mem_attn(mq.reshape(b_p, seq, MEM_WIDTH), mem_kt, mem_vt, 0, TQ).reshape(b_p * seq, MEM_WIDTH)
        h_p = _out_proj(olat, odiff, omem, sg, h_p, w_uv_l, dn_l, w_out_l, g_l, bb_l, diff_gain)
        heads_last = lambda a, n: a.reshape(b_p, DIFF_HEADS, DIFF_HEAD_DIM, n).transpose(0, 3, 1, 2)
        new_p.append((ckv.reshape(b_p, seq, KV_LORA), krt.transpose(0, 2, 1),
                      heads_last(dkt, seq), heads_last(dvt, seq),
                      heads_last(mem_kt, n_mem), heads_last(mem_vt, n_mem)))

        (qlat, qrope, ckv, krt, _, dq, dkt, dvt, _, _, mq, sg) = _proj(h_s, 1, tabs_s, *proj_w)
        kr_s = krt[0].T.reshape(nb, dec, MLA_ROPE)
        dk_s = dkt[0].T.reshape(nb, dec, DIFF_WIDTH)
        dv_s = dvt[0].T.reshape(nb, dec, DIFF_WIDTH)
        ckv_s = ckv.reshape(nb, dec, KV_LORA)
        q_s = qlat.reshape(MLA_HEADS, nb, dec, KV_LORA).transpose(1, 0, 2, 3).reshape(nb, MLA_HEADS * dec, KV_LORA)
        qr_s = (qrope.reshape(nb, dec, MLA_HEADS, MLA_ROPE).transpose(0, 2, 1, 3)
                .reshape(nb, MLA_HEADS * dec, MLA_ROPE))
        olat_s = _mla_sample(page_table, q_s, qr_s, ckv_s, kr_s, cache_mla_ckv, cache_krt, l)
        olat_s = (olat_s.reshape(nb, MLA_HEADS, dec, KV_LORA).transpose(1, 0, 2, 3)
                  .reshape(MLA_HEADS, rows_s, KV_LORA))
        dq3 = dq.reshape(nb, 1, dec, DIFF_WIDTH)
        qd_s = jnp.where(grp_lane[None, None, None, :] == jnp.arange(N_GRP)[None, :, None, None],
                         dq3, jnp.zeros_like(dq3)).reshape(nb, N_GRP * dec, DIFF_WIDTH)
        odiff_s = _diff_sample(page_table, lamv, qd_s, dk_s, dv_s, bias_s_past, bias_s_new, cache_dkt,
                               cache_dvt, l, lam_init).reshape(rows_s, DIFF_WIDTH)
        omem_s = _mem_attn(mq.reshape(nb, dec, MEM_WIDTH), cache_mkt, cache_mvt, l, dec).reshape(rows_s, MEM_WIDTH)
        h_s = _out_proj(olat_s, odiff_s, omem_s, sg, h_s, w_uv_l, dn_l, w_out_l, g_l, bb_l, diff_gain)
        new_s.append((ckv_s, kr_s, dk_s.reshape(nb, dec, DIFF_HEADS, DIFF_HEAD_DIM),
                      dv_s.reshape(nb, dec, DIFF_HEADS, DIFF_HEAD_DIM)))

    p_out = [jnp.stack(a, axis=1) for a in zip(*new_p)]
    s_out = [jnp.stack(a, axis=1) for a in zip(*new_s)]
    return (h_p.reshape(b_p, seq, D_MODEL), h_s.reshape(nb, dec, D_MODEL), *p_out, *s_out)
```
